```python
import jax, jax.numpy as jnp
from jax import lax
import numpy as np

D_MODEL = 1024
BATCH = 16
SEQ = 2048
DEPTH = 1

W_A = D_MODEL
W_B = D_MODEL
CONV_WIDTH = 31
POOL_WINDOWS = (2, 4, 8, 16)
N_POOL_GROUPS = len(POOL_WINDOWS)
POOL_GROUP = W_B // N_POOL_GROUPS
IN_COLS = 3 * W_A + 2 * W_B + 2 * D_MODEL
EPS = 1e-6

kernel_name = "hybrid_conformer_pool_gated_block"


def rmsnorm(x, g):
    xf = x.astype(jnp.float32)
    y = xf * lax.rsqrt(jnp.mean(xf * xf, axis=-1, keepdims=True) + EPS)
    return (y * g.astype(jnp.float32)).astype(x.dtype)


def layernorm(x, g, b):
    xf = x.astype(jnp.float32)
    mu = jnp.mean(xf, axis=-1, keepdims=True)
    var = jnp.mean(jnp.square(xf - mu), axis=-1, keepdims=True)
    y = (xf - mu) * lax.rsqrt(var + EPS)
    return (y * g.astype(jnp.float32) + b.astype(jnp.float32)).astype(x.dtype)


def conformer_branch(a_val, a_glu, a_z, conv_w, conv_b, ln_g, ln_b, w_proj_a):
    u = a_val * jax.nn.sigmoid(a_glu)
    rhs = conv_w.reshape(CONV_WIDTH, 1, W_A).astype(u.dtype)
    c = lax.conv_general_dilated(
        u, rhs, window_strides=(1,), padding=[(CONV_WIDTH - 1, 0)],
        dimension_numbers=("NWC", "WIO", "NWC"), feature_group_count=W_A)
    c = c + conv_b
    c = layernorm(c, ln_g, ln_b)
    c = jax.nn.silu(c) * jax.nn.silu(a_z)
    return c @ w_proj_a


def causal_mean(v, w):
    S = v.shape[1]
    vf = v.astype(jnp.float32)
    cs = lax.cumsum(vf, axis=1)
    cs_pad = jnp.pad(cs, ((0, 0), (1, 0), (0, 0)))
    lower = jnp.pad(cs_pad, ((0, 0), (w, 0), (0, 0)))[:, :S + 1]
    count = jnp.minimum(jnp.arange(1, S + 1), w).astype(jnp.float32)[None, :, None]
    return ((cs_pad[:, 1:] - lower[:, 1:]) / count).astype(v.dtype)


def pooling_branch(b_val, b_z, w_pool, b_pool, pool_scale, w_proj_b):
    B, S, _ = b_val.shape
    groups = []
    for gi, w in enumerate(POOL_WINDOWS):
        v = b_val[..., gi * POOL_GROUP:(gi + 1) * POOL_GROUP]
        groups.append(causal_mean(v, w) - v)
    p = jnp.stack(groups, axis=2)
    p = jnp.einsum("bsgc,gcd->bsgd", p, w_pool) + b_pool
    p = p.reshape(B, S, W_B) * pool_scale
    p = p * jax.nn.silu(b_z)
    return p @ w_proj_b


def setup_inputs(seed: int = 0) -> dict:
    key = jax.random.key(seed)
    ks = jax.random.split(key, 16)
    f = jnp.float32
    nrm = lambda k, shape, s: jax.random.normal(k, shape, f) * s
    return {
        "x": jax.random.normal(ks[0], (BATCH, SEQ, D_MODEL), f),
        "norm_g": 1.0 + nrm(ks[1], (DEPTH, D_MODEL), 0.02),
        "w_in": nrm(ks[2], (DEPTH, D_MODEL, IN_COLS), D_MODEL ** -0.5),
        "b_in": nrm(ks[3], (DEPTH, IN_COLS), 0.02),
        "conv_w": nrm(ks[4], (DEPTH, CONV_WIDTH, W_A), CONV_WIDTH ** -0.5),
        "conv_b": nrm(ks[5], (DEPTH, W_A), 0.02),
        "ln_g": 1.0 + nrm(ks[6], (DEPTH, W_A), 0.02),
        "ln_b": nrm(ks[7], (DEPTH, W_A), 0.02),
        "w_proj_a": nrm(ks[8], (DEPTH, W_A, D_MODEL), W_A ** -0.5),
        "w_pool": nrm(ks[9], (DEPTH, N_POOL_GROUPS, POOL_GROUP, POOL_GROUP), POOL_GROUP ** -0.5),
        "b_pool": nrm(ks[10], (DEPTH, N_POOL_GROUPS, POOL_GROUP), 0.02),
        "pool_scale": 1.0 + nrm(ks[11], (DEPTH, W_B), 0.02),
        "w_proj_b": nrm(ks[12], (DEPTH, W_B, D_MODEL), W_B ** -0.5),
        "w_out": nrm(ks[13], (DEPTH, D_MODEL, D_MODEL), D_MODEL ** -0.5),
        "norm_f_g": 1.0 + nrm(ks[14], (D_MODEL,), 0.02),
    }


def reference(x, norm_g, w_in, b_in, conv_w, conv_b, ln_g, ln_b, w_proj_a,
              w_pool, b_pool, pool_scale, w_proj_b, w_out, norm_f_g):
    o1 = W_A
    o2 = o1 + W_A
    o3 = o2 + W_A
    o4 = o3 + W_B
    o5 = o4 + W_B
    o6 = o5 + D_MODEL
    for l in range(DEPTH):
        h = rmsnorm(x, norm_g[l])
        proj = h @ w_in[l] + b_in[l]
        a_val, a_glu, a_z = proj[..., :o1], proj[..., o1:o2], proj[..., o2:o3]
        b_val, b_z = proj[..., o3:o4], proj[..., o4:o5]
        r_a, r_b = proj[..., o5:o6], proj[..., o6:]
        y_a = conformer_branch(a_val, a_glu, a_z, conv_w[l], conv_b[l],
                               ln_g[l], ln_b[l], w_proj_a[l])
        y_b = pooling_branch(b_val, b_z, w_pool[l], b_pool[l], pool_scale[l], w_proj_b[l])
        merged = jax.nn.sigmoid(r_a) * y_a + jax.nn.sigmoid(r_b) * y_b
        x = x + merged @ w_out[l]
    return rmsnorm(x, norm_f_g)
```

```python
import functools

import jax
import jax.numpy as jnp
from jax import lax
from jax.experimental import pallas as pl
from jax.experimental.pallas import tpu as pltpu

EPS = 1e-6
CONV_WIDTH = 31
POOL_WINDOWS = (2, 4, 8, 16)
N_IN_BLOCKS = 7

V7X_SUBLANES = 8
V7X_LANES = 128
V7X_VMEM_BYTES = 64 * 1024 * 1024

ROWS = 256
CONV_SUB = 32
CONV_HALO = 32
POOL_HALO = 16


def _sigmoid(x):
    return 1.0 / (1.0 + jnp.exp(-x))


def _silu(x):
    return x * _sigmoid(x)


def _block_kernel(x_ref, norm_g_ref, w_in_ref, b_in_ref, conv_w_ref, conv_b_ref,
                  ln_g_ref, ln_b_ref, w_proj_a_ref, w_pool_ref, b_pool_ref,
                  pool_scale_ref, w_proj_b_ref, w_out_ref, norm_f_g_ref,
                  o_ref, u_ext, v_ext, c_scr):
    rows, d = x_ref.shape
    group = d // len(POOL_WINDOWS)
    seq_tile = pl.program_id(1)

    @pl.when(seq_tile == 0)
    def _():
        u_ext[0:CONV_HALO, :] = jnp.zeros((CONV_HALO, d), jnp.float32)
        v_ext[0:POOL_HALO, :] = jnp.zeros((POOL_HALO, d), jnp.float32)

    x = x_ref[...]
    ms = jnp.mean(x * x, axis=-1, keepdims=True)
    h = (x * lax.rsqrt(ms + EPS) * norm_g_ref[...]).astype(jnp.bfloat16)

    def proj(j):
        cols = slice(j * d, (j + 1) * d)
        return (jnp.dot(h, w_in_ref[:, cols], preferred_element_type=jnp.float32)
                + b_in_ref[:, cols])

    u_ext[CONV_HALO:, :] = proj(0) * _sigmoid(proj(1))
    first = CONV_HALO - (CONV_WIDTH - 1)
    for i in range(rows // CONV_SUB):
        acc = None
        for k in range(CONV_WIDTH):
            term = u_ext[pl.ds(i * CONV_SUB + first + k, CONV_SUB), :] * conv_w_ref[k:k + 1, :]
            acc = term if acc is None else acc + term
        c_scr[i * CONV_SUB:(i + 1) * CONV_SUB, :] = acc + conv_b_ref[...]
    u_ext[0:CONV_HALO, :] = u_ext[rows:rows + CONV_HALO, :]

    c = c_scr[...]
    mu = jnp.mean(c, axis=-1, keepdims=True)
    cc = c - mu
    var = jnp.mean(cc * cc, axis=-1, keepdims=True)
    c = cc * lax.rsqrt(var + EPS) * ln_g_ref[...] + ln_b_ref[...]
    gated_a = (_silu(c) * _silu(proj(2))).astype(jnp.bfloat16)
    y_a = jnp.dot(gated_a, w_proj_a_ref[...], preferred_element_type=jnp.float32)

    v_ext[POOL_HALO:, :] = proj(3)
    pos1 = (seq_tile * rows + 1
            + lax.broadcasted_iota(jnp.int32, (rows, V7X_LANES), 0))
    pooled = []
    for gi, w in enumerate(POOL_WINDOWS):
        cols = slice(gi * group, (gi + 1) * group)
        tok = v_ext[POOL_HALO:, cols]
        tot = tok
        for back in range(1, w):
            tot = tot + v_ext[pl.ds(POOL_HALO - back, rows), cols]
        inv_count = 1.0 / jnp.minimum(pos1, w).astype(jnp.float32)
        inv_count = jnp.concatenate([inv_count] * (group // V7X_LANES), axis=1)
        p_in = (tot * inv_count - tok).astype(jnp.bfloat16)
        pooled.append(jnp.dot(p_in, w_pool_ref[gi], preferred_element_type=jnp.float32))
    v_ext[0:POOL_HALO, :] = v_ext[rows:rows + POOL_HALO, :]
    p = (jnp.concatenate(pooled, axis=1) + b_pool_ref[...]) * pool_scale_ref[...]
    gated_b = (p * _silu(proj(4))).astype(jnp.bfloat16)
    y_b = jnp.dot(gated_b, w_proj_b_ref[...], preferred_element_type=jnp.float32)

    merged = (_sigmoid(proj(5)) * y_a + _sigmoid(proj(6)) * y_b).astype(jnp.bfloat16)
    out = x + jnp.dot(merged, w_out_ref[...], preferred_element_type=jnp.float32)
    ms_out = jnp.mean(out * out, axis=-1, keepdims=True)
    o_ref[...] = out * lax.rsqrt(ms_out + EPS) * norm_f_g_ref[...]


def _resident(shape):
    zeros = (0,) * len(shape)
    return pl.BlockSpec(shape, lambda b, s: zeros, pipeline_mode=pl.Buffered(1))


def _vmem_limit_bytes(d, rows):
    bf16, f32 = 2, 4
    weights = (d * N_IN_BLOCKS * d + 3 * d * d + d * d // len(POOL_WINDOWS)) * bf16
    vectors = 16 * N_IN_BLOCKS * d * f32 + CONV_HALO * d * f32
    tiles = 2 * 2 * rows * d * f32
    scratch = (3 * rows + CONV_HALO + POOL_HALO) * d * f32
    temporaries = 24 * rows * d * f32
    return min(weights + vectors + tiles + scratch + temporaries, V7X_VMEM_BYTES * 7 // 8)


def kernel(x, norm_g, w_in, b_in, conv_w, conv_b, ln_g, ln_b, w_proj_a, w_pool, b_pool,
           pool_scale, w_proj_b, w_out, norm_f_g):
    batch, seq, d = x.shape
    assert norm_g.shape[0] == 1, "the final RMSNorm is fused into the single layer"
    assert seq % ROWS == 0 and ROWS % CONV_SUB == 0 and d % (len(POOL_WINDOWS) * V7X_LANES) == 0
    bf16 = jnp.bfloat16
    row = lambda v: v.reshape(1, -1)

    call = pl.pallas_call(
        _block_kernel,
        grid=(batch, seq // ROWS),
        in_specs=[
            pl.BlockSpec((None, ROWS, d), lambda b, s: (b, s, 0)),
            _resident((1, d)),
            _resident((d, N_IN_BLOCKS * d)),
            _resident((1, N_IN_BLOCKS * d)),
            _resident((CONV_WIDTH, d)),
            _resident((1, d)),
            _resident((1, d)),
            _resident((1, d)),
            _resident((d, d)),
            _resident(w_pool.shape[1:]),
            _resident((1, d)),
            _resident((1, d)),
            _resident((d, d)),
            _resident((d, d)),
            _resident((1, d)),
        ],
        out_specs=pl.BlockSpec((None, ROWS, d), lambda b, s: (b, s, 0)),
        out_shape=jax.ShapeDtypeStruct((batch, seq, d), x.dtype),
        scratch_shapes=[
            pltpu.VMEM((CONV_HALO + ROWS, d), jnp.float32),
            pltpu.VMEM((POOL_HALO + ROWS, d), jnp.float32),
            pltpu.VMEM((ROWS, d), jnp.float32),
        ],
        compiler_params=pltpu.CompilerParams(
            dimension_semantics=("arbitrary", "arbitrary"),
            vmem_limit_bytes=_vmem_limit_bytes(d, ROWS)),
        name="hybrid_block",
    )
    return call(x, row(norm_g[0]), w_in[0].astype(bf16), row(b_in[0]), conv_w[0],
                row(conv_b[0]), row(ln_g[0]), row(ln_b[0]), w_proj_a[0].astype(bf16),
                w_pool[0].astype(bf16), row(b_pool[0]), row(pool_scale[0]),
                w_proj_b[0].astype(bf16), w_out[0].astype(bf16), row(norm_f_g))
```

```python
import jax
import jax.numpy as jnp
from jax import lax
from jax.experimental import pallas as pl
from jax.experimental.pallas import tpu as pltpu

EPS = 1e-6
CONV_WIDTH = 31
POOL_WINDOWS = (2, 4, 8, 16)
N_IN_BLOCKS = 7

V7X_LANES = 128
V7X_VMEM_BYTES = 64 * 1024 * 1024

ROWS = 256
SHIFT_ROWS = 32
CONV_HALO = 32
POOL_HALO = 16


def _sigmoid(x):
    return 1.0 / (1.0 + jnp.exp(-x))


def _silu(x):
    return x * _sigmoid(x)


def _lane_tile(t):
    return slice(t * V7X_LANES, (t + 1) * V7X_LANES)


def _block_kernel(x_ref, norm_g_ref, w_in_ref, b_in_ref, conv_w_ref, conv_b_ref,
                  ln_g_ref, ln_b_ref, w_proj_a_ref, w_pool_ref, b_pool_ref,
                  pool_scale_ref, w_proj_b_ref, w_out_ref, norm_f_g_ref,
                  o_ref, u_ext, v_ext, c_scr, tot_scr):
    rows, d = x_ref.shape
    n_tiles = d // V7X_LANES
    tiles_per_group = n_tiles // len(POOL_WINDOWS)
    seq_tile = pl.program_id(1)

    @pl.when(seq_tile == 0)
    def _():
        u_ext[:, 0:CONV_HALO, :] = jnp.zeros((n_tiles, CONV_HALO, V7X_LANES), jnp.float32)
        v_ext[:, 0:POOL_HALO, :] = jnp.zeros((n_tiles, POOL_HALO, V7X_LANES), jnp.float32)

    def bf16_weight(packed):
        return pltpu.bitcast(packed, jnp.bfloat16)

    x = x_ref[...]
    ms = jnp.mean(x * x, axis=-1, keepdims=True)
    h = (x * lax.rsqrt(ms + EPS) * norm_g_ref[...]).astype(jnp.bfloat16)

    def proj(j):
        cols = slice(j * d, (j + 1) * d)
        return (jnp.dot(h, bf16_weight(w_in_ref[:, cols]), preferred_element_type=jnp.float32)
                + b_in_ref[:, cols])

    u = proj(0) * _sigmoid(proj(1))
    for t in range(n_tiles):
        u_ext[t, CONV_HALO:, :] = u[:, _lane_tile(t)]
    first = CONV_HALO - (CONV_WIDTH - 1)
    for t in range(n_tiles):
        bias = conv_b_ref[:, _lane_tile(t)]
        for r0 in range(0, rows, 2 * SHIFT_ROWS):
            acc = [None, None]
            for k in range(CONV_WIDTH):
                w_k = conv_w_ref[k:k + 1, _lane_tile(t)]
                for par in range(2):
                    window = u_ext[t, pl.ds(first + r0 + par + k, SHIFT_ROWS, stride=2), :]
                    acc[par] = window * w_k if acc[par] is None else acc[par] + window * w_k
            for par in range(2):
                c_scr[t, pl.ds(r0 + par, SHIFT_ROWS, stride=2), :] = acc[par] + bias
        u_ext[t, 0:CONV_HALO, :] = u_ext[t, rows:rows + CONV_HALO, :]

    c = jnp.concatenate([c_scr[t] for t in range(n_tiles)], axis=1)
    mu = jnp.mean(c, axis=-1, keepdims=True)
    cc = c - mu
    var = jnp.mean(cc * cc, axis=-1, keepdims=True)
    c = cc * lax.rsqrt(var + EPS) * ln_g_ref[...] + ln_b_ref[...]
    gated_a = (_silu(c) * _silu(proj(2))).astype(jnp.bfloat16)
    y_a = jnp.dot(gated_a, bf16_weight(w_proj_a_ref[...]), preferred_element_type=jnp.float32)

    b_val = proj(3)
    for t in range(n_tiles):
        v_ext[t, POOL_HALO:, :] = b_val[:, _lane_tile(t)]
    pos1 = (seq_tile * rows + 1
            + lax.broadcasted_iota(jnp.int32, (rows, V7X_LANES), 0))
    pooled = []
    for gi, w in enumerate(POOL_WINDOWS):
        inv_count = 1.0 / jnp.minimum(pos1, w).astype(jnp.float32)
        p_in = []
        for t in range(gi * tiles_per_group, (gi + 1) * tiles_per_group):
            for r0 in range(0, rows, 2 * SHIFT_ROWS):
                for par in range(2):
                    tot = None
                    for back in range(w):
                        window = v_ext[t, pl.ds(POOL_HALO + r0 + par - back, SHIFT_ROWS, stride=2), :]
                        tot = window if tot is None else tot + window
                    tot_scr[t, pl.ds(r0 + par, SHIFT_ROWS, stride=2), :] = tot
            p_in.append(tot_scr[t] * inv_count - v_ext[t, POOL_HALO:, :])
            v_ext[t, 0:POOL_HALO, :] = v_ext[t, rows:rows + POOL_HALO, :]
        p_in = jnp.concatenate(p_in, axis=1).astype(jnp.bfloat16)
        pooled.append(jnp.dot(p_in, bf16_weight(w_pool_ref[gi]),
                              preferred_element_type=jnp.float32))
    p = (jnp.concatenate(pooled, axis=1) + b_pool_ref[...]) * pool_scale_ref[...]
    gated_b = (p * _silu(proj(4))).astype(jnp.bfloat16)
    y_b = jnp.dot(gated_b, bf16_weight(w_proj_b_ref[...]), preferred_element_type=jnp.float32)

    merged = (_sigmoid(proj(5)) * y_a + _sigmoid(proj(6)) * y_b).astype(jnp.bfloat16)
    out = x + jnp.dot(merged, bf16_weight(w_out_ref[...]), preferred_element_type=jnp.float32)
    ms_out = jnp.mean(out * out, axis=-1, keepdims=True)
    o_ref[...] = out * lax.rsqrt(ms_out + EPS) * norm_f_g_ref[...]


def _resident(shape):
    zeros = (0,) * len(shape)
    return pl.BlockSpec(shape, lambda b, s: zeros, pipeline_mode=pl.Buffered(1))


def _pack_bf16_rows(w):
    *lead, k, n = w.shape
    pairs = w.astype(jnp.bfloat16).reshape(*lead, k // 2, 2, n)
    return lax.bitcast_convert_type(jnp.swapaxes(pairs, -1, -2), jnp.uint32)


def _vmem_limit_bytes(d, rows):
    bf16, f32 = 2, 4
    weights = (d * N_IN_BLOCKS * d + 3 * d * d + d * d // len(POOL_WINDOWS)) * bf16
    vectors = 16 * N_IN_BLOCKS * d * f32 + CONV_HALO * d * f32
    tiles = 2 * 2 * rows * d * f32
    scratch = (4 * rows + CONV_HALO + POOL_HALO) * d * f32
    temporaries = 24 * rows * d * f32
    return min(weights + vectors + tiles + scratch + temporaries, V7X_VMEM_BYTES * 7 // 8)


def kernel(x, norm_g, w_in, b_in, conv_w, conv_b, ln_g, ln_b, w_proj_a, w_pool, b_pool,
           pool_scale, w_proj_b, w_out, norm_f_g):
    batch, seq, d = x.shape
    assert norm_g.shape[0] == 1, "the final RMSNorm is fused into the single layer"
    assert seq % ROWS == 0 and ROWS % (2 * SHIFT_ROWS) == 0
    assert d % (len(POOL_WINDOWS) * V7X_LANES) == 0
    n_tiles = d // V7X_LANES
    row = lambda v: v.reshape(1, -1)
    w_pool_packed = _pack_bf16_rows(w_pool[0])

    call = pl.pallas_call(
        _block_kernel,
        grid=(batch, seq // ROWS),
        in_specs=[
            pl.BlockSpec((None, ROWS, d), lambda b, s: (b, s, 0)),
            _resident((1, d)),
            _resident((d // 2, N_IN_BLOCKS * d)),
            _resident((1, N_IN_BLOCKS * d)),
            _resident((CONV_WIDTH, d)),
            _resident((1, d)),
            _resident((1, d)),
            _resident((1, d)),
            _resident((d // 2, d)),
            _resident(w_pool_packed.shape),
            _resident((1, d)),
            _resident((1, d)),
            _resident((d // 2, d)),
            _resident((d // 2, d)),
            _resident((1, d)),
        ],
        out_specs=pl.BlockSpec((None, ROWS, d), lambda b, s: (b, s, 0)),
        out_shape=jax.ShapeDtypeStruct((batch, seq, d), x.dtype),
        scratch_shapes=[
            pltpu.VMEM((n_tiles, CONV_HALO + ROWS, V7X_LANES), jnp.float32),
            pltpu.VMEM((n_tiles, POOL_HALO + ROWS, V7X_LANES), jnp.float32),
            pltpu.VMEM((n_tiles, ROWS, V7X_LANES), jnp.float32),
            pltpu.VMEM((n_tiles, ROWS, V7X_LANES), jnp.float32),
        ],
        compiler_params=pltpu.CompilerParams(
            dimension_semantics=("arbitrary", "arbitrary"),
            vmem_limit_bytes=_vmem_limit_bytes(d, ROWS)),
        name="hybrid_block",
    )
    return call(x, row(norm_g[0]), _pack_bf16_rows(w_in[0]), row(b_in[0]), conv_w[0],
                row(conv_b[0]), row(ln_g[0]), row(ln_b[0]), _pack_bf16_rows(w_proj_a[0]),
                w_pool_packed, row(b_pool[0]), row(pool_scale[0]),
                _pack_bf16_rows(w_proj_b[0]), _pack_bf16_rows(w_out[0]), row(norm_f_g))
```

```python
import jax
import jax.numpy as jnp
from jax import lax
from jax.experimental import pallas as pl
from jax.experimental.pallas import tpu as pltpu

EPS = 1e-6
CONV_WIDTH = 31
POOL_WINDOWS = (2, 4, 8, 16)
N_IN_BLOCKS = 7
A_VAL, A_GLU, A_Z, B_VAL, B_Z, R_A, R_B = range(N_IN_BLOCKS)

V7X_LANES = 128
V7X_MXU_COLS = 256
V7X_VMEM_BYTES = 64 * 1024 * 1024

ROWS = 256
CHUNK = 32
PHASES = 4
PHASE_ROWS = 16
CONV_HALO = 32
POOL_HALO = 16
PIECE_COLS = 2 * V7X_MXU_COLS
PACK_COLS = 1024


def _sigmoid(x):
    return 1.0 / (1.0 + jnp.exp(-x))


def _silu(x):
    return x * _sigmoid(x)


def _lane_tile(t):
    return slice(t * V7X_LANES, (t + 1) * V7X_LANES)


def _interleave(major, minor):
    done = 0
    for i, thunk in enumerate(major):
        thunk()
        upto = (i + 1) * len(minor) // len(major)
        for other in minor[done:upto]:
            other()
        done = upto
    for other in minor[done:]:
        other()


def _block_kernel(x_ref, norm_g_ref, w_in_ref, b_in_ref, conv_w_ref, conv_b_ref,
                  ln_g_ref, ln_b_ref, w_proj_a_ref, w_pool_ref, b_pool_ref,
                  pool_scale_ref, w_proj_b_ref, w_out_ref, norm_f_g_ref,
                  o_ref,
                  h_scr, z_scr, u_ext, v_ext, c_scr, m_scr, lhs_scr, pin_scr, p_scr, y_scr):
    rows, d = x_ref.shape
    n_tiles = d // V7X_LANES
    tiles_per_group = n_tiles // len(POOL_WINDOWS)
    group = d // len(POOL_WINDOWS)
    seq_tile = pl.program_id(1)
    row_blocks = range(0, rows, PHASES * PHASE_ROWS)
    chunks = [slice(r, r + CHUNK) for r in range(0, rows, CHUNK)]
    piece_cols = [slice(c, c + PIECE_COLS) for c in range(0, d, PIECE_COLS)]
    GATED_A, GATED_B, MERGED = range(3)
    Y_A, Y_B = range(2)

    @pl.when(seq_tile == 0)
    def _():
        u_ext[:, 0:CONV_HALO, :] = jnp.zeros((n_tiles, CONV_HALO, V7X_LANES), jnp.float32)
        v_ext[:, 0:POOL_HALO, :] = jnp.zeros((n_tiles, POOL_HALO, V7X_LANES), jnp.float32)

    def matmul(lhs, packed_rhs):
        return jnp.dot(lhs, pltpu.bitcast(packed_rhs, jnp.bfloat16),
                       preferred_element_type=jnp.float32)

    def in_proj(j, cols):
        wcols = slice(j * d + cols.start, j * d + cols.stop)
        return matmul(h_scr[...], w_in_ref[:, wcols]) + b_in_ref[:, wcols]

    def rms_in(rc):
        x = x_ref[rc, :]
        ms = jnp.mean(x * x, axis=-1, keepdims=True)
        h_scr[rc, :] = (x * lax.rsqrt(ms + EPS) * norm_g_ref[...]).astype(jnp.bfloat16)

    def proj_piece(j, cols):
        z_scr[j, :, cols] = in_proj(j, cols)

    def b_val_piece(cols):
        b_val = in_proj(B_VAL, cols)
        for i, t in enumerate(range(cols.start // V7X_LANES, cols.stop // V7X_LANES)):
            v_ext[t, POOL_HALO:, :] = b_val[:, _lane_tile(i)]

    def glu(rc):
        u = z_scr[A_VAL, rc, :] * _sigmoid(z_scr[A_GLU, rc, :])
        for t in range(n_tiles):
            u_ext[t, CONV_HALO + rc.start:CONV_HALO + rc.stop, :] = u[:, _lane_tile(t)]

    first = CONV_HALO - (CONV_WIDTH - 1)

    def conv_block(t, r0):
        acc = [None] * PHASES
        for k in range(CONV_WIDTH):
            w_k = conv_w_ref[k:k + 1, _lane_tile(t)]
            for ph in range(PHASES):
                window = u_ext[t, pl.ds(first + r0 + ph + k, PHASE_ROWS, stride=PHASES), :]
                acc[ph] = window * w_k if acc[ph] is None else acc[ph] + window * w_k
        for ph in range(PHASES):
            c_scr[t, pl.ds(r0 + ph, PHASE_ROWS, stride=PHASES), :] = (
                acc[ph] + conv_b_ref[:, _lane_tile(t)])

    def conv_carry(t):
        u_ext[t, 0:CONV_HALO, :] = u_ext[t, rows:rows + CONV_HALO, :]

    def pool_block(t, r0):
        w = POOL_WINDOWS[t // tiles_per_group]
        for ph in range(PHASES):
            tot = None
            for back in range(w):
                window = v_ext[t, pl.ds(POOL_HALO + r0 + ph - back, PHASE_ROWS, stride=PHASES), :]
                tot = window if tot is None else tot + window
            m_scr[t, pl.ds(r0 + ph, PHASE_ROWS, stride=PHASES), :] = tot

    def pool_finish(t):
        w = POOL_WINDOWS[t // tiles_per_group]
        pos1 = (seq_tile * rows + 1
                + lax.broadcasted_iota(jnp.int32, (rows, V7X_LANES), 0))
        inv_count = 1.0 / jnp.minimum(pos1, w).astype(jnp.float32)
        centred = m_scr[t] * inv_count - v_ext[t, POOL_HALO:, :]
        pin_scr[:, _lane_tile(t)] = centred.astype(jnp.bfloat16)
        v_ext[t, 0:POOL_HALO, :] = v_ext[t, rows:rows + POOL_HALO, :]

    def gate_a(rc):
        c = jnp.concatenate([c_scr[t, rc, :] for t in range(n_tiles)], axis=1)
        mu = jnp.mean(c, axis=-1, keepdims=True)
        cc = c - mu
        var = jnp.mean(cc * cc, axis=-1, keepdims=True)
        c = cc * lax.rsqrt(var + EPS) * ln_g_ref[...] + ln_b_ref[...]
        lhs_scr[GATED_A, rc, :] = (_silu(c) * _silu(z_scr[A_Z, rc, :])).astype(jnp.bfloat16)

    def pool_map(gi):
        cols = slice(gi * group, (gi + 1) * group)
        p = matmul(pin_scr[:, cols], w_pool_ref[gi])
        p_scr[:, cols] = (p + b_pool_ref[:, cols]) * pool_scale_ref[:, cols]

    def gate_b(rc):
        lhs_scr[GATED_B, rc, :] = (p_scr[rc, :] * _silu(z_scr[B_Z, rc, :])).astype(jnp.bfloat16)

    def branch_proj(slot, src, w_ref, cols):
        y_scr[slot, :, cols] = matmul(lhs_scr[src], w_ref[:, cols])

    def merge_gates(rc):
        y_scr[Y_A, rc, :] = _sigmoid(z_scr[R_A, rc, :]) * y_scr[Y_A, rc, :]
        z_scr[R_B, rc, :] = _sigmoid(z_scr[R_B, rc, :])

    def merge(rc):
        merged = y_scr[Y_A, rc, :] + z_scr[R_B, rc, :] * y_scr[Y_B, rc, :]
        lhs_scr[MERGED, rc, :] = merged.astype(jnp.bfloat16)

    def out_piece(cols):
        o_ref[:, cols] = matmul(lhs_scr[MERGED], w_out_ref[:, cols])

    def rms_out(rc):
        out = x_ref[rc, :] + o_ref[rc, :]
        ms = jnp.mean(out * out, axis=-1, keepdims=True)
        o_ref[rc, :] = out * lax.rsqrt(ms + EPS) * norm_f_g_ref[...]

    for rc in chunks:
        rms_in(rc)
    for cols in piece_cols:
        proj_piece(A_VAL, cols)
        proj_piece(A_GLU, cols)
    for rc in chunks:
        glu(rc)

    pieces = [lambda cols=cols: b_val_piece(cols) for cols in piece_cols]
    for j in (A_Z, B_Z, R_A, R_B):
        pieces += [lambda j=j, cols=cols: proj_piece(j, cols) for cols in piece_cols]
    vector_work = []
    for t in range(n_tiles):
        vector_work += [lambda t=t, r0=r0: conv_block(t, r0) for r0 in row_blocks]
        vector_work.append(lambda t=t: conv_carry(t))
    for t in range(n_tiles):
        vector_work += [lambda t=t, r0=r0: pool_block(t, r0) for r0 in row_blocks]
        vector_work.append(lambda t=t: pool_finish(t))
    vector_work += [lambda rc=rc: gate_a(rc) for rc in chunks]
    _interleave(pieces, vector_work)

    for gi in range(len(POOL_WINDOWS)):
        pool_map(gi)
    _interleave([lambda cols=cols: branch_proj(Y_A, GATED_A, w_proj_a_ref, cols)
                 for cols in piece_cols],
                [lambda rc=rc: gate_b(rc) for rc in chunks])
    _interleave([lambda cols=cols: branch_proj(Y_B, GATED_B, w_proj_b_ref, cols)
                 for cols in piece_cols],
                [lambda rc=rc: merge_gates(rc) for rc in chunks])
    for rc in chunks:
        merge(rc)
    for cols in piece_cols:
        out_piece(cols)
    for rc in chunks:
        rms_out(rc)


def _pack_kernel(w_ref, o_ref):
    o_ref[...] = pltpu.bitcast(w_ref[...].astype(jnp.bfloat16), jnp.uint32)


def _pack_bf16_rows(w):
    k, n = w.shape
    cols = min(n, PACK_COLS)
    return pl.pallas_call(
        _pack_kernel,
        grid=(n // cols,),
        in_specs=[pl.BlockSpec((k, cols), lambda j: (0, j))],
        out_specs=pl.BlockSpec((k // 2, cols), lambda j: (0, j)),
        out_shape=jax.ShapeDtypeStruct((k // 2, n), jnp.uint32),
        compiler_params=pltpu.CompilerParams(dimension_semantics=("arbitrary",)),
        name="pack_bf16_rows",
    )(w)


def _resident(shape):
    zeros = (0,) * len(shape)
    return pl.BlockSpec(shape, lambda b, s: zeros, pipeline_mode=pl.Buffered(1))


def _scratch_shapes(d, rows):
    n_tiles = d // V7X_LANES
    f32, bf16 = jnp.float32, jnp.bfloat16
    return [
        pltpu.VMEM((rows, d), bf16),
        pltpu.VMEM((N_IN_BLOCKS, rows, d), f32),
        pltpu.VMEM((n_tiles, CONV_HALO + rows, V7X_LANES), f32),
        pltpu.VMEM((n_tiles, POOL_HALO + rows, V7X_LANES), f32),
        pltpu.VMEM((n_tiles, rows, V7X_LANES), f32),
        pltpu.VMEM((n_tiles, rows, V7X_LANES), f32),
        pltpu.VMEM((3, rows, d), bf16),
        pltpu.VMEM((rows, d), bf16),
        pltpu.VMEM((rows, d), f32),
        pltpu.VMEM((2, rows, d), f32),
    ]


def _vmem_limit_bytes(d, rows):
    bf16, f32 = 2, 4
    weights = (d * N_IN_BLOCKS * d + 3 * d * d + d * d // len(POOL_WINDOWS)) * bf16
    vectors = 16 * N_IN_BLOCKS * d * f32 + CONV_HALO * d * f32
    tiles = 2 * 2 * rows * d * f32
    scratch = ((N_IN_BLOCKS + 7) * rows + CONV_HALO + POOL_HALO) * d * f32 + 5 * rows * d * bf16
    temporaries = 8 * rows * d * f32
    return min(weights + vectors + tiles + scratch + temporaries, V7X_VMEM_BYTES * 7 // 8)


def kernel(x, norm_g, w_in, b_in, conv_w, conv_b, ln_g, ln_b, w_proj_a, w_pool, b_pool,
           pool_scale, w_proj_b, w_out, norm_f_g):
    batch, seq, d = x.shape
    assert norm_g.shape[0] == 1, "the final RMSNorm is fused into the single layer"
    assert seq % ROWS == 0 and ROWS % (PHASES * PHASE_ROWS) == 0 and ROWS % CHUNK == 0
    assert d % (len(POOL_WINDOWS) * V7X_LANES) == 0 and d % PIECE_COLS == 0
    n_groups, group, _ = w_pool.shape[1:]
    row = lambda v: v.reshape(1, -1)
    w_pool_packed = _pack_bf16_rows(w_pool[0].reshape(n_groups * group, group))
    w_pool_packed = w_pool_packed.reshape(n_groups, group // 2, group)

    call = pl.pallas_call(
        _block_kernel,
        grid=(batch, seq // ROWS),
        in_specs=[
            pl.BlockSpec((None, ROWS, d), lambda b, s: (b, s, 0)),
            _resident((1, d)),
            _resident((d // 2, N_IN_BLOCKS * d)),
            _resident((1, N_IN_BLOCKS * d)),
            _resident((CONV_WIDTH, d)),
            _resident((1, d)),
            _resident((1, d)),
            _resident((1, d)),
            _resident((d // 2, d)),
            _resident(w_pool_packed.shape),
            _resident((1, d)),
            _resident((1, d)),
            _resident((d // 2, d)),
            _resident((d // 2, d)),
            _resident((1, d)),
        ],
        out_specs=pl.BlockSpec((None, ROWS, d), lambda b, s: (b, s, 0)),
        out_shape=jax.ShapeDtypeStruct((batch, seq, d), x.dtype),
        scratch_shapes=_scratch_shapes(d, ROWS),
        compiler_params=pltpu.CompilerParams(
            dimension_semantics=("arbitrary", "arbitrary"),
            vmem_limit_bytes=_vmem_limit_bytes(d, ROWS)),
        name="hybrid_block",
    )
    return call(x, row(norm_g[0]), _pack_bf16_rows(w_in[0]), row(b_in[0]), conv_w[0],
                row(conv_b[0]), row(ln_g[0]), row(ln_b[0]), _pack_bf16_rows(w_proj_a[0]),
                w_pool_packed, row(b_pool[0]), row(pool_scale[0]),
                _pack_bf16_rows(w_proj_b[0]), _pack_bf16_rows(w_out[0]), row(norm_f_g))
```

```python
import jax
import jax.numpy as jnp
from jax import lax
from jax.experimental import pallas as pl
from jax.experimental.pallas import tpu as pltpu

EPS = 1e-6
CONV_WIDTH = 31
POOL_WINDOWS = (2, 4, 8, 16)
N_IN_BLOCKS = 7
A_VAL, A_GLU, A_Z, B_VAL, B_Z, R_A, R_B = range(N_IN_BLOCKS)
EARLY = (A_VAL, A_GLU, B_VAL)
REST = (A_Z, B_Z, R_A, R_B)

V7X_LANES = 128
V7X_VMEM_BYTES = 64 * 1024 * 1024

ROWS = 256
CHUNK = 32
PHASES = 4
PHASE_ROWS = 16
SHIFT_BLOCK = PHASES * PHASE_ROWS
CONV_HALO = 32
POOL_HALO = 16
PIECE_COLS = 512
CONV_TRIPS = len(REST)


def _sigmoid(x):
    return 1.0 / (1.0 + jnp.exp(-x))


def _silu(x):
    return x * _sigmoid(x)


def _lane_tile(t):
    return slice(t * V7X_LANES, (t + 1) * V7X_LANES)


def _block_kernel(x_ref, norm_g_ref, w_early_ref, w_rest_ref, b_in_ref, conv_w_ref, conv_b_ref,
                  ln_g_ref, ln_b_ref, w_proj_a_ref, w_pool_ref, b_pool_ref,
                  pool_scale_ref, w_proj_b_ref, w_out_ref, norm_f_g_ref,
                  o_ref,
                  h_scr, z_early, z_rest, u_ext, v_ext, c_scr, m_scr, lhs_scr, pin_scr, p_scr,
                  y_scr):
    rows, d = x_ref.shape
    n_tiles = d // V7X_LANES
    tiles_per_trip = n_tiles // CONV_TRIPS
    tiles_per_group = n_tiles // len(POOL_WINDOWS)
    group = d // len(POOL_WINDOWS)
    seq_tile = pl.program_id(1)
    row_blocks = range(0, rows, SHIFT_BLOCK)
    chunks = [slice(r, r + CHUNK) for r in range(0, rows, CHUNK)]
    piece_cols = [slice(c, c + PIECE_COLS) for c in range(0, d, PIECE_COLS)]
    GATED_A, GATED_B, MERGED = range(3)
    Y_A, Y_B = range(2)

    @pl.when(seq_tile == 0)
    def _():
        u_ext[:, 0:CONV_HALO, :] = jnp.zeros((n_tiles, CONV_HALO, V7X_LANES), jnp.float32)
        v_ext[:, 0:POOL_HALO, :] = jnp.zeros((n_tiles, POOL_HALO, V7X_LANES), jnp.float32)

    def matmul(lhs, packed_rhs):
        return jnp.dot(lhs, pltpu.bitcast(packed_rhs, jnp.bfloat16),
                       preferred_element_type=jnp.float32)

    def rest(j, rc):
        return z_rest[REST.index(j), rc, :] + b_in_ref[:, j * d:(j + 1) * d]

    def rms_in(rc):
        x = x_ref[rc, :]
        ms = jnp.mean(x * x, axis=-1, keepdims=True)
        h_scr[rc, :] = (x * lax.rsqrt(ms + EPS) * norm_g_ref[...]).astype(jnp.bfloat16)

    def early_piece(j, cols):
        slot = EARLY.index(j)
        wcols = slice(slot * d + cols.start, slot * d + cols.stop)
        bcols = slice(j * d + cols.start, j * d + cols.stop)
        z_early[slot, :, cols] = matmul(h_scr[...], w_early_ref[:, wcols]) + b_in_ref[:, bcols]

    def glu(rc):
        u = z_early[EARLY.index(A_VAL), rc, :] * _sigmoid(z_early[EARLY.index(A_GLU), rc, :])
        for t in range(n_tiles):
            u_ext[t, CONV_HALO + rc.start:CONV_HALO + rc.stop, :] = u[:, _lane_tile(t)]

    def pool_fill(rc):
        b_val = z_early[EARLY.index(B_VAL), rc, :]
        for t in range(n_tiles):
            v_ext[t, POOL_HALO + rc.start:POOL_HALO + rc.stop, :] = b_val[:, _lane_tile(t)]

    first = CONV_HALO - (CONV_WIDTH - 1)

    def conv_block(t, r0):
        acc = [None] * PHASES
        for k in range(CONV_WIDTH):
            w_k = conv_w_ref[t, k:k + 1, :]
            for ph in range(PHASES):
                window = u_ext[t, pl.ds(first + r0 + ph + k, PHASE_ROWS, stride=PHASES), :]
                acc[ph] = window * w_k if acc[ph] is None else acc[ph] + window * w_k
        for ph in range(PHASES):
            c_scr[t, pl.ds(r0 + ph, PHASE_ROWS, stride=PHASES), :] = acc[ph] + conv_b_ref[t]

    def pool_block(t, r0):
        w = POOL_WINDOWS[t // tiles_per_group]
        for ph in range(PHASES):
            tot = None
            for back in range(w):
                window = v_ext[t, pl.ds(POOL_HALO + r0 + ph - back, PHASE_ROWS, stride=PHASES), :]
                tot = window if tot is None else tot + window
            m_scr[t, pl.ds(r0 + ph, PHASE_ROWS, stride=PHASES), :] = tot
        span = pl.ds(r0, SHIFT_BLOCK)
        pos1 = (seq_tile * rows + r0 + 1
                + lax.broadcasted_iota(jnp.int32, (SHIFT_BLOCK, V7X_LANES), 0))
        inv_count = 1.0 / jnp.minimum(pos1, w).astype(jnp.float32)
        centred = m_scr[t, span, :] * inv_count - v_ext[t, pl.ds(POOL_HALO + r0, SHIFT_BLOCK), :]
        pin_scr[span, _lane_tile(t)] = centred.astype(jnp.bfloat16)

    def conv_trip(c, carry):
        z_rest[c] = matmul(h_scr[...], w_rest_ref[c])
        for i in range(tiles_per_trip):
            t = c * tiles_per_trip + i
            for r0 in row_blocks:
                conv_block(t, r0)
            u_ext[t, 0:CONV_HALO, :] = u_ext[t, rows:rows + CONV_HALO, :]
        r0 = pl.multiple_of(c * SHIFT_BLOCK, SHIFT_BLOCK)
        for t in range(n_tiles):
            pool_block(t, r0)
        return carry

    def pool_carry(t):
        v_ext[t, 0:POOL_HALO, :] = v_ext[t, rows:rows + POOL_HALO, :]

    def gate_a(rc):
        c = jnp.concatenate([c_scr[t, rc, :] for t in range(n_tiles)], axis=1)
        mu = jnp.mean(c, axis=-1, keepdims=True)
        cc = c - mu
        var = jnp.mean(cc * cc, axis=-1, keepdims=True)
        c = cc * lax.rsqrt(var + EPS) * ln_g_ref[...] + ln_b_ref[...]
        lhs_scr[GATED_A, rc, :] = (_silu(c) * _silu(rest(A_Z, rc))).astype(jnp.bfloat16)

    def pool_map(gi):
        cols = slice(gi * group, (gi + 1) * group)
        p = matmul(pin_scr[:, cols], w_pool_ref[gi])
        p_scr[:, cols] = (p + b_pool_ref[:, cols]) * pool_scale_ref[:, cols]

    def gate_b(rc):
        lhs_scr[GATED_B, rc, :] = (p_scr[rc, :] * _silu(rest(B_Z, rc))).astype(jnp.bfloat16)

    def branch_proj(slot, src, w_ref, cols):
        y_scr[slot, :, cols] = matmul(lhs_scr[src], w_ref[:, cols])

    def merge(rc):
        merged = (_sigmoid(rest(R_A, rc)) * y_scr[Y_A, rc, :]
                  + _sigmoid(rest(R_B, rc)) * y_scr[Y_B, rc, :])
        lhs_scr[MERGED, rc, :] = merged.astype(jnp.bfloat16)

    def out_piece(cols):
        o_ref[:, cols] = matmul(lhs_scr[MERGED], w_out_ref[:, cols])

    def rms_out(rc):
        out = x_ref[rc, :] + o_ref[rc, :]
        ms = jnp.mean(out * out, axis=-1, keepdims=True)
        o_ref[rc, :] = out * lax.rsqrt(ms + EPS) * norm_f_g_ref[...]

    for rc in chunks:
        rms_in(rc)
    for cols in piece_cols:
        for j in EARLY:
            early_piece(j, cols)
    for rc in chunks:
        glu(rc)
        pool_fill(rc)
    lax.fori_loop(0, CONV_TRIPS, conv_trip, 0)
    for t in range(n_tiles):
        pool_carry(t)
    for rc in chunks:
        gate_a(rc)
    for gi in range(len(POOL_WINDOWS)):
        pool_map(gi)
    for cols in piece_cols:
        branch_proj(Y_A, GATED_A, w_proj_a_ref, cols)
    for rc in chunks:
        gate_b(rc)
    for cols in piece_cols:
        branch_proj(Y_B, GATED_B, w_proj_b_ref, cols)
    for rc in chunks:
        merge(rc)
    for cols in piece_cols:
        out_piece(cols)
    for rc in chunks:
        rms_out(rc)


def _pack_kernel(ids_ref, w_ref, o_ref):
    del ids_ref
    o_ref[...] = pltpu.bitcast(w_ref[...].astype(jnp.bfloat16), jnp.uint32)


def _pack_bf16_rows(w, cols, col_blocks, stacked):
    k, n_blocks = w.shape[0], len(col_blocks)
    if stacked:
        out_shape, out_block = (n_blocks, k // 2, cols), (None, k // 2, cols)
        out_map = lambda i, ids: (i, 0, 0)
    else:
        out_shape, out_block = (k // 2, n_blocks * cols), (k // 2, cols)
        out_map = lambda i, ids: (0, i)
    return pl.pallas_call(
        _pack_kernel,
        grid_spec=pltpu.PrefetchScalarGridSpec(
            num_scalar_prefetch=1,
            grid=(n_blocks,),
            in_specs=[pl.BlockSpec((k, cols), lambda i, ids: (0, ids[i]))],
            out_specs=pl.BlockSpec(out_block, out_map)),
        out_shape=jax.ShapeDtypeStruct(out_shape, jnp.uint32),
        compiler_params=pltpu.CompilerParams(dimension_semantics=("arbitrary",)),
        name="pack_bf16_rows",
    )(jnp.asarray(col_blocks, jnp.int32), w)


def _resident(shape):
    zeros = (0,) * len(shape)
    return pl.BlockSpec(shape, lambda b, s: zeros, pipeline_mode=pl.Buffered(1))


def _scratch_shapes(d, rows):
    n_tiles = d // V7X_LANES
    f32, bf16 = jnp.float32, jnp.bfloat16
    return [
        pltpu.VMEM((rows, d), bf16),
        pltpu.VMEM((len(EARLY), rows, d), f32),
        pltpu.VMEM((len(REST), rows, d), f32),
        pltpu.VMEM((n_tiles, CONV_HALO + rows, V7X_LANES), f32),
        pltpu.VMEM((n_tiles, POOL_HALO + rows, V7X_LANES), f32),
        pltpu.VMEM((n_tiles, rows, V7X_LANES), f32),
        pltpu.VMEM((n_tiles, rows, V7X_LANES), f32),
        pltpu.VMEM((3, rows, d), bf16),
        pltpu.VMEM((rows, d), bf16),
        pltpu.VMEM((rows, d), f32),
        pltpu.VMEM((2, rows, d), f32),
    ]


def _vmem_limit_bytes(d, rows):
    bf16, f32 = 2, 4
    weights = (d * N_IN_BLOCKS * d + 3 * d * d + d * d // len(POOL_WINDOWS)) * bf16
    vectors = 16 * N_IN_BLOCKS * d * f32 + CONV_HALO * d * f32
    tiles = 2 * 2 * rows * d * f32
    scratch = ((N_IN_BLOCKS + 7) * rows + CONV_HALO + POOL_HALO) * d * f32 + 5 * rows * d * bf16
    temporaries = 8 * rows * d * f32
    return min(weights + vectors + tiles + scratch + temporaries, V7X_VMEM_BYTES * 7 // 8)


def kernel(x, norm_g, w_in, b_in, conv_w, conv_b, ln_g, ln_b, w_proj_a, w_pool, b_pool,
           pool_scale, w_proj_b, w_out, norm_f_g):
    batch, seq, d = x.shape
    assert norm_g.shape[0] == 1, "the final RMSNorm is fused into the single layer"
    assert seq % ROWS == 0 and ROWS % CHUNK == 0 and d % PIECE_COLS == 0
    assert ROWS == CONV_TRIPS * SHIFT_BLOCK, "each conv-loop trip pools one shifted-window block"
    n_tiles = d // V7X_LANES
    assert n_tiles % CONV_TRIPS == 0 and n_tiles % len(POOL_WINDOWS) == 0
    n_groups, group, _ = w_pool.shape[1:]
    row = lambda v: v.reshape(1, -1)

    w_early = _pack_bf16_rows(w_in[0], d, EARLY, stacked=False)
    w_rest = _pack_bf16_rows(w_in[0], d, REST, stacked=True)
    w_pool_packed = _pack_bf16_rows(w_pool[0].reshape(n_groups * group, group), group, (0,),
                                    stacked=False).reshape(n_groups, group // 2, group)
    pack_square = lambda w: _pack_bf16_rows(w, d, (0,), stacked=False)
    conv_w_tiles = conv_w[0].reshape(CONV_WIDTH, n_tiles, V7X_LANES).transpose(1, 0, 2)
    conv_b_tiles = conv_b[0].reshape(n_tiles, 1, V7X_LANES)

    call = pl.pallas_call(
        _block_kernel,
        grid=(batch, seq // ROWS),
        in_specs=[
            pl.BlockSpec((None, ROWS, d), lambda b, s: (b, s, 0)),
            _resident((1, d)),
            _resident(w_early.shape),
            _resident(w_rest.shape),
            _resident((1, N_IN_BLOCKS * d)),
            _resident(conv_w_tiles.shape),
            _resident(conv_b_tiles.shape),
            _resident((1, d)),
            _resident((1, d)),
            _resident((d // 2, d)),
            _resident(w_pool_packed.shape),
            _resident((1, d)),
            _resident((1, d)),
            _resident((d // 2, d)),
            _resident((d // 2, d)),
            _resident((1, d)),
        ],
        out_specs=pl.BlockSpec((None, ROWS, d), lambda b, s: (b, s, 0)),
        out_shape=jax.ShapeDtypeStruct((batch, seq, d), x.dtype),
        scratch_shapes=_scratch_shapes(d, ROWS),
        compiler_params=pltpu.CompilerParams(
            dimension_semantics=("arbitrary", "arbitrary"),
            vmem_limit_bytes=_vmem_limit_bytes(d, ROWS)),
        name="hybrid_block",
    )
    return call(x, row(norm_g[0]), w_early, w_rest, row(b_in[0]), conv_w_tiles, conv_b_tiles,
                row(ln_g[0]), row(ln_b[0]), pack_square(w_proj_a[0]),
                w_pool_packed, row(b_pool[0]), row(pool_scale[0]),
                pack_square(w_proj_b[0]), pack_square(w_out[0]), row(norm_f_g))
```

```python
import jax
import jax.numpy as jnp
from jax import lax
from jax.experimental import pallas as pl
from jax.experimental.pallas import tpu as pltpu

EPS = 1e-6
CONV_WIDTH = 31
POOL_WINDOWS = (2, 4, 8, 16)
N_IN_BLOCKS = 7
A_VAL, A_GLU, A_Z, B_VAL, B_Z, R_A, R_B = range(N_IN_BLOCKS)

V7X_LANES = 128
V7X_MXU_COLS = 256
V7X_VMEM_BYTES = 64 * 1024 * 1024

ROWS = 256
CHUNK = 32
PHASES = 4
PHASE_ROWS = 16
CONV_HALO = 32
POOL_HALO = 16
PIECE_COLS = V7X_MXU_COLS
PACK_COLS = 1024


def _sigmoid(x):
    return 1.0 / (1.0 + jnp.exp(-x))


def _silu(x):
    return x * _sigmoid(x)


def _lane_tile(t):
    return slice(t * V7X_LANES, (t + 1) * V7X_LANES)


def _block_kernel(x_ref, norm_g_ref, w_in_ref, b_in_ref, conv_w_ref, conv_b_ref,
                  ln_g_ref, ln_b_ref, w_proj_a_ref, w_pool_ref, b_pool_ref,
                  pool_scale_ref, w_proj_b_ref, w_out_ref, norm_f_g_ref,
                  o_ref,
                  h_scr, z_scr, u_pk, v_ext, c_scr, m_scr, lhs_scr, pin_scr, p_scr, y_scr):
    rows, d = x_ref.shape
    n_tiles = d // V7X_LANES
    n_pairs = n_tiles // 2
    tiles_per_group = n_tiles // len(POOL_WINDOWS)
    group = d // len(POOL_WINDOWS)
    seq_tile = pl.program_id(1)
    row_blocks = range(0, rows, PHASES * PHASE_ROWS)
    chunks = [slice(r, r + CHUNK) for r in range(0, rows, CHUNK)]
    piece_cols = [slice(c, c + PIECE_COLS) for c in range(0, d, PIECE_COLS)]
    GATED_A, GATED_B, MERGED = range(3)
    Y_A, Y_B = range(2)

    @pl.when(seq_tile == 0)
    def _():
        u_pk[:, 0:CONV_HALO, :] = jnp.zeros((n_pairs, CONV_HALO, V7X_LANES), jnp.uint32)
        v_ext[:, 0:POOL_HALO, :] = jnp.zeros((n_tiles, POOL_HALO, V7X_LANES), jnp.float32)

    def matmul(lhs, packed_rhs):
        return jnp.dot(lhs, pltpu.bitcast(packed_rhs, jnp.bfloat16),
                       preferred_element_type=jnp.float32)

    def in_proj(j, cols):
        wcols = slice(j * d + cols.start, j * d + cols.stop)
        return matmul(h_scr[...], w_in_ref[:, wcols]) + b_in_ref[:, wcols]

    def rms_in(rc):
        x = x_ref[rc, :]
        ms = jnp.mean(x * x, axis=-1, keepdims=True)
        h_scr[rc, :] = (x * lax.rsqrt(ms + EPS) * norm_g_ref[...]).astype(jnp.bfloat16)

    def proj_piece(j, cols):
        z_scr[j, :, cols] = in_proj(j, cols)

    def b_val_piece(cols):
        b_val = in_proj(B_VAL, cols)
        for i, t in enumerate(range(cols.start // V7X_LANES, cols.stop // V7X_LANES)):
            v_ext[t, POOL_HALO:, :] = b_val[:, _lane_tile(i)]

    def glu(rc):
        u = z_scr[A_VAL, rc, :] * _sigmoid(z_scr[A_GLU, rc, :])
        for pair in range(n_pairs):
            u_pk[pair, CONV_HALO + rc.start:CONV_HALO + rc.stop, :] = pltpu.pack_elementwise(
                [u[:, _lane_tile(2 * pair)], u[:, _lane_tile(2 * pair + 1)]],
                packed_dtype=jnp.bfloat16)

    first = CONV_HALO - (CONV_WIDTH - 1)

    def conv_taps(pair):
        taps = []
        for k in range(CONV_WIDTH):
            w_k = pltpu.pack_elementwise(
                [conv_w_ref[k:k + 1, _lane_tile(2 * pair + i)] for i in range(2)],
                packed_dtype=jnp.bfloat16)
            taps.append(pltpu.bitcast(jnp.broadcast_to(w_k, (PHASE_ROWS, V7X_LANES)),
                                      jnp.bfloat16))
        return taps

    def conv_block(pair, r0, taps):
        acc = [[None, None] for _ in range(PHASES)]
        for k in range(CONV_WIDTH):
            for ph in range(PHASES):
                window = u_pk[pair, pl.ds(first + r0 + ph + k, PHASE_ROWS, stride=PHASES), :]
                term = pltpu.bitcast(window, jnp.bfloat16) * taps[k]
                prev = acc[ph][k % 2]
                acc[ph][k % 2] = term if prev is None else prev + term
        for ph in range(PHASES):
            for i in range(2):
                t = 2 * pair + i
                halves = [pltpu.unpack_elementwise(
                    pltpu.bitcast(a, jnp.uint32), index=i, packed_dtype=jnp.bfloat16,
                    unpacked_dtype=jnp.float32) for a in acc[ph]]
                c_scr[t, pl.ds(r0 + ph, PHASE_ROWS, stride=PHASES), :] = (
                    halves[0] + halves[1] + conv_b_ref[:, _lane_tile(t)])

    def conv_carry(pair):
        u_pk[pair, 0:CONV_HALO, :] = u_pk[pair, rows:rows + CONV_HALO, :]

    def pool_block(t, r0):
        w = POOL_WINDOWS[t // tiles_per_group]
        for ph in range(PHASES):
            tot = None
            for back in range(w):
                window = v_ext[t, pl.ds(POOL_HALO + r0 + ph - back, PHASE_ROWS, stride=PHASES), :]
                tot = window if tot is None else tot + window
            m_scr[t, pl.ds(r0 + ph, PHASE_ROWS, stride=PHASES), :] = tot

    def pool_finish(t):
        w = POOL_WINDOWS[t // tiles_per_group]
        pos1 = (seq_tile * rows + 1
                + lax.broadcasted_iota(jnp.int32, (rows, V7X_LANES), 0))
        inv_count = 1.0 / jnp.minimum(pos1, w).astype(jnp.float32)
        centred = m_scr[t] * inv_count - v_ext[t, POOL_HALO:, :]
        pin_scr[:, _lane_tile(t)] = centred.astype(jnp.bfloat16)
        v_ext[t, 0:POOL_HALO, :] = v_ext[t, rows:rows + POOL_HALO, :]

    def gate_a(rc):
        c = jnp.concatenate([c_scr[t, rc, :] for t in range(n_tiles)], axis=1)
        mu = jnp.mean(c, axis=-1, keepdims=True)
        cc = c - mu
        var = jnp.mean(cc * cc, axis=-1, keepdims=True)
        c = cc * lax.rsqrt(var + EPS) * ln_g_ref[...] + ln_b_ref[...]
        lhs_scr[GATED_A, rc, :] = (_silu(c) * _silu(z_scr[A_Z, rc, :])).astype(jnp.bfloat16)

    def pool_map(gi):
        cols = slice(gi * group, (gi + 1) * group)
        p = matmul(pin_scr[:, cols], w_pool_ref[gi])
        p_scr[:, cols] = (p + b_pool_ref[:, cols]) * pool_scale_ref[:, cols]

    def gate_b(rc):
        lhs_scr[GATED_B, rc, :] = (p_scr[rc, :] * _silu(z_scr[B_Z, rc, :])).astype(jnp.bfloat16)

    def branch_proj(slot, src, w_ref, cols):
        y_scr[slot, :, cols] = matmul(lhs_scr[src], w_ref[:, cols])

    def merge_gates(rc):
        y_scr[Y_A, rc, :] = _sigmoid(z_scr[R_A, rc, :]) * y_scr[Y_A, rc, :]
        z_scr[R_B, rc, :] = _sigmoid(z_scr[R_B, rc, :])

    def merge(rc):
        merged = y_scr[Y_A, rc, :] + z_scr[R_B, rc, :] * y_scr[Y_B, rc, :]
        lhs_scr[MERGED, rc, :] = merged.astype(jnp.bfloat16)

    def out_piece(cols):
        o_ref[:, cols] = matmul(lhs_scr[MERGED], w_out_ref[:, cols])

    def rms_out(rc):
        out = x_ref[rc, :] + o_ref[rc, :]
        ms = jnp.mean(out * out, axis=-1, keepdims=True)
        o_ref[rc, :] = out * lax.rsqrt(ms + EPS) * norm_f_g_ref[...]

    for rc in chunks:
        rms_in(rc)
    for cols in piece_cols:
        proj_piece(A_VAL, cols)
        proj_piece(A_GLU, cols)
    for rc in chunks:
        glu(rc)
    for cols in piece_cols:
        b_val_piece(cols)
    for j in (A_Z, B_Z, R_A, R_B):
        for cols in piece_cols:
            proj_piece(j, cols)
    for pair in range(n_pairs):
        taps = conv_taps(pair)
        for r0 in row_blocks:
            conv_block(pair, r0, taps)
        conv_carry(pair)
    for t in range(n_tiles):
        for r0 in row_blocks:
            pool_block(t, r0)
        pool_finish(t)
    for rc in chunks:
        gate_a(rc)
    for gi in range(len(POOL_WINDOWS)):
        pool_map(gi)
    for cols in piece_cols:
        branch_proj(Y_A, GATED_A, w_proj_a_ref, cols)
    for rc in chunks:
        gate_b(rc)
    for cols in piece_cols:
        branch_proj(Y_B, GATED_B, w_proj_b_ref, cols)
    for rc in chunks:
        merge_gates(rc)
    for rc in chunks:
        merge(rc)
    for cols in piece_cols:
        out_piece(cols)
    for rc in chunks:
        rms_out(rc)


def _pack_kernel(w_ref, o_ref):
    o_ref[...] = pltpu.bitcast(w_ref[...].astype(jnp.bfloat16), jnp.uint32)


def _pack_bf16_rows(w):
    k, n = w.shape
    cols = min(n, PACK_COLS)
    return pl.pallas_call(
        _pack_kernel,
        grid=(n // cols,),
        in_specs=[pl.BlockSpec((k, cols), lambda j: (0, j))],
        out_specs=pl.BlockSpec((k // 2, cols), lambda j: (0, j)),
        out_shape=jax.ShapeDtypeStruct((k // 2, n), jnp.uint32),
        compiler_params=pltpu.CompilerParams(dimension_semantics=("arbitrary",)),
        name="pack_bf16_rows",
    )(w)


def _resident(shape):
    zeros = (0,) * len(shape)
    return pl.BlockSpec(shape, lambda b, s: zeros, pipeline_mode=pl.Buffered(1))


def _scratch_shapes(d, rows):
    n_tiles = d // V7X_LANES
    f32, bf16 = jnp.float32, jnp.bfloat16
    return [
        pltpu.VMEM((rows, d), bf16),
        pltpu.VMEM((N_IN_BLOCKS, rows, d), f32),
        pltpu.VMEM((n_tiles // 2, CONV_HALO + rows, V7X_LANES), jnp.uint32),
        pltpu.VMEM((n_tiles, POOL_HALO + rows, V7X_LANES), f32),
        pltpu.VMEM((n_tiles, rows, V7X_LANES), f32),
        pltpu.VMEM((n_tiles, rows, V7X_LANES), f32),
        pltpu.VMEM((3, rows, d), bf16),
        pltpu.VMEM((rows, d), bf16),
        pltpu.VMEM((rows, d), f32),
        pltpu.VMEM((2, rows, d), f32),
    ]


def _vmem_limit_bytes(d, rows):
    bf16, f32 = 2, 4
    weights = (d * N_IN_BLOCKS * d + 3 * d * d + d * d // len(POOL_WINDOWS)) * bf16
    vectors = 16 * N_IN_BLOCKS * d * f32 + CONV_HALO * d * f32
    tiles = 2 * 2 * rows * d * f32
    scratch = ((N_IN_BLOCKS + 7) * rows + CONV_HALO + POOL_HALO) * d * f32 + 5 * rows * d * bf16
    temporaries = 8 * rows * d * f32
    return min(weights + vectors + tiles + scratch + temporaries, V7X_VMEM_BYTES * 7 // 8)


def kernel(x, norm_g, w_in, b_in, conv_w, conv_b, ln_g, ln_b, w_proj_a, w_pool, b_pool,
           pool_scale, w_proj_b, w_out, norm_f_g):
    batch, seq, d = x.shape
    assert norm_g.shape[0] == 1, "the final RMSNorm is fused into the single layer"
    assert seq % ROWS == 0 and ROWS % (PHASES * PHASE_ROWS) == 0 and ROWS % CHUNK == 0
    assert d % (len(POOL_WINDOWS) * V7X_LANES) == 0 and d % PIECE_COLS == 0
    n_groups, group, _ = w_pool.shape[1:]
    row = lambda v: v.reshape(1, -1)
    w_pool_packed = _pack_bf16_rows(w_pool[0].reshape(n_groups * group, group))
    w_pool_packed = w_pool_packed.reshape(n_groups, group // 2, group)

    call = pl.pallas_call(
        _block_kernel,
        grid=(batch, seq // ROWS),
        in_specs=[
            pl.BlockSpec((None, ROWS, d), lambda b, s: (b, s, 0)),
            _resident((1, d)),
            _resident((d // 2, N_IN_BLOCKS * d)),
            _resident((1, N_IN_BLOCKS * d)),
            _resident((CONV_WIDTH, d)),
            _resident((1, d)),
            _resident((1, d)),
            _resident((1, d)),
            _resident((d // 2, d)),
            _resident(w_pool_packed.shape),
            _resident((1, d)),
            _resident((1, d)),
            _resident((d // 2, d)),
            _resident((d // 2, d)),
            _resident((1, d)),
        ],
        out_specs=pl.BlockSpec((None, ROWS, d), lambda b, s: (b, s, 0)),
        out_shape=jax.ShapeDtypeStruct((batch, seq, d), x.dtype),
        scratch_shapes=_scratch_shapes(d, ROWS),
        compiler_params=pltpu.CompilerParams(
            dimension_semantics=("arbitrary", "arbitrary"),
            vmem_limit_bytes=_vmem_limit_bytes(d, ROWS)),
        name="hybrid_block",
    )
    return call(x, row(norm_g[0]), _pack_bf16_rows(w_in[0]), row(b_in[0]), conv_w[0],
                row(conv_b[0]), row(ln_g[0]), row(ln_b[0]), _pack_bf16_rows(w_proj_a[0]),
                w_pool_packed, row(b_pool[0]), row(pool_scale[0]),
                _pack_bf16_rows(w_proj_b[0]), _pack_bf16_rows(w_out[0]), row(norm_f_g))
```
